```python
import jax, jax.numpy as jnp
from jax import lax
import numpy as np

D_MODEL = 1024
BATCH = 1
SEQ = 16384
DEPTH = 1

CHUNK = 64
PLE_DIM = 256
GLA_HEADS = 4
GLA_DK = 64
GLA_DV = 128
GLA_WIDTH = GLA_HEADS * GLA_DV
GLA_QK = GLA_HEADS * GLA_DK
GATE_RANK = 16
GATE_TAU = 16.0
CONV_WIDTH = 512
CONV_K = 3
D_MIX = GLA_WIDTH + CONV_WIDTH
SPLITS = [GLA_QK, GLA_QK, GLA_WIDTH, GLA_WIDTH, GATE_RANK,
          CONV_WIDTH, CONV_WIDTH, CONV_WIDTH, CONV_WIDTH]
D_IN = int(sum(SPLITS))
EPS = 1e-6

kernel_name = "hybrid_gla_shortconv_block"


def _rmsnorm(x, w):
    xf = x.astype(jnp.float32)
    y = xf * lax.rsqrt(jnp.mean(xf * xf, axis=-1, keepdims=True) + EPS)
    return (y * w.astype(jnp.float32)).astype(x.dtype)


def _gla_chunked(q, k, v, log_a):
    b, s, h, dk = q.shape
    dv = v.shape[-1]
    nc = s // CHUNK
    qc = q.reshape(b, nc, CHUNK, h, dk).astype(jnp.float32)
    kc = k.reshape(b, nc, CHUNK, h, dk).astype(jnp.float32)
    vc = v.reshape(b, nc, CHUNK, h, dv).astype(jnp.float32)
    cum = jnp.cumsum(log_a.reshape(b, nc, CHUNK, h, dk).astype(jnp.float32), axis=2)
    cum_end = cum[:, :, -1:]
    k_dec = kc * jnp.exp(cum_end - cum)
    gamma = jnp.exp(cum_end[:, :, 0])
    d_state = jnp.einsum('bnchk,bnchv->bnhkv', k_dec, vc)

    def step(state, inp):
        g, ds = inp
        state = g[..., None] * state + ds
        return state, state

    init = jnp.zeros((b, h, dk, dv), jnp.float32)
    _, states = lax.scan(step, init, (jnp.moveaxis(gamma, 1, 0), jnp.moveaxis(d_state, 1, 0)))
    states = jnp.moveaxis(states, 0, 1)
    o = jnp.einsum('bnchk,bnhkv->bnchv', qc, states)
    return o.reshape(b, s, h, dv).astype(q.dtype)


def _causal_depthwise_conv(u, w, bias):
    up = jnp.pad(u, ((0, 0), (CONV_K - 1, 0), (0, 0)))
    s = u.shape[1]
    out = bias
    for j in range(CONV_K):
        out = out + w[j] * up[:, j:j + s]
    return out


def setup_inputs(seed: int = 0) -> dict:
    key = jax.random.key(seed)
    ks = jax.random.split(key, 16)
    f32 = jnp.float32
    x = jax.random.normal(ks[0], (BATCH, SEQ, D_MODEL), f32)
    p = jax.random.normal(ks[1], (DEPTH, BATCH, SEQ, PLE_DIM), f32)
    norm_w = 1.0 + 0.02 * jax.random.normal(ks[2], (DEPTH, D_MODEL), f32)
    w_in = jax.random.normal(ks[3], (DEPTH, D_MODEL, D_IN), f32) * D_MODEL ** -0.5
    w_alpha_up = jax.random.normal(ks[4], (DEPTH, GATE_RANK, GLA_QK), f32) * GATE_RANK ** -0.5
    b_alpha = 0.5 + 0.1 * jax.random.normal(ks[5], (DEPTH, GLA_QK), f32)
    gla_norm_w = 1.0 + 0.02 * jax.random.normal(ks[6], (DEPTH, GLA_DV), f32)
    conv_w = jax.random.normal(ks[7], (DEPTH, CONV_K, CONV_WIDTH), f32) * CONV_K ** -0.5
    conv_b = 0.02 * jax.random.normal(ks[8], (DEPTH, CONV_WIDTH), f32)
    w_out = jax.random.normal(ks[9], (DEPTH, D_MIX, D_MODEL), f32) * D_MIX ** -0.5
    w_ple = jax.random.normal(ks[10], (DEPTH, PLE_DIM, D_MODEL), f32) * PLE_DIM ** -0.5
    w_ple_gate = jax.random.normal(ks[11], (DEPTH, D_MODEL, D_MODEL), f32) * D_MODEL ** -0.5
    b_ple_gate = 0.02 * jax.random.normal(ks[12], (DEPTH, D_MODEL), f32)
    final_norm_w = 1.0 + 0.02 * jax.random.normal(ks[13], (D_MODEL,), f32)
    return {"x": x, "p": p, "norm_w": norm_w, "w_in": w_in, "w_alpha_up": w_alpha_up,
            "b_alpha": b_alpha, "gla_norm_w": gla_norm_w, "conv_w": conv_w, "conv_b": conv_b,
            "w_out": w_out, "w_ple": w_ple, "w_ple_gate": w_ple_gate, "b_ple_gate": b_ple_gate,
            "final_norm_w": final_norm_w}


def reference(x, p, norm_w, w_in, w_alpha_up, b_alpha, gla_norm_w, conv_w, conv_b,
              w_out, w_ple, w_ple_gate, b_ple_gate, final_norm_w):
    b, s, _ = x.shape
    idx = list(np.cumsum(SPLITS)[:-1])
    h = x
    for i in range(DEPTH):
        n = _rmsnorm(h, norm_w[i])
        z = n @ w_in[i]
        q, k, v, g_gla, a_low, cb, cc, cu, g_conv = jnp.split(z, idx, axis=-1)

        log_a = jax.nn.log_sigmoid((a_low @ w_alpha_up[i] + b_alpha[i]).astype(jnp.float32)) / GATE_TAU
        qh = q.reshape(b, s, GLA_HEADS, GLA_DK) * (GLA_DK ** -0.5)
        kh = k.reshape(b, s, GLA_HEADS, GLA_DK)
        vh = v.reshape(b, s, GLA_HEADS, GLA_DV)
        o_gla = _gla_chunked(qh, kh, vh, log_a.reshape(b, s, GLA_HEADS, GLA_DK))
        o_gla = _rmsnorm(o_gla, gla_norm_w[i]).reshape(b, s, GLA_WIDTH)
        y_gla = o_gla * jax.nn.silu(g_gla)

        y_conv = cb * _causal_depthwise_conv(cc * cu, conv_w[i], conv_b[i])
        y_conv = y_conv * jax.nn.silu(g_conv)

        h = h + jnp.concatenate([y_gla, y_conv], axis=-1) @ w_out[i]

        gate = jax.nn.sigmoid(h @ w_ple_gate[i] + b_ple_gate[i])
        h = h + gate * (p[i] @ w_ple[i])
    return _rmsnorm(h, final_norm_w)
```

```python
import functools

import jax
import jax.numpy as jnp
from jax import lax
from jax.experimental import pallas as pl
from jax.experimental.pallas import tpu as pltpu

D_MODEL = 1024
CHUNK = 64
PLE_DIM = 256
GLA_HEADS = 4
GLA_DK = 64
GLA_DV = 128
GLA_WIDTH = GLA_HEADS * GLA_DV
GLA_QK = GLA_HEADS * GLA_DK
GATE_RANK = 16
GATE_TAU = 16.0
CONV_WIDTH = 512
CONV_K = 3
D_MIX = GLA_WIDTH + CONV_WIDTH
EPS = 1e-6

LANES = 128
SUBLANES = 8
A_PAD = LANES

C_Q = 0
C_K = C_Q + GLA_QK
C_V = C_K + GLA_QK
C_G = C_V + GLA_WIDTH
C_CB = C_G + GLA_WIDTH
C_CC = C_CB + CONV_WIDTH
C_CU = C_CC + CONV_WIDTH
C_GC = C_CU + CONV_WIDTH
C_A = C_GC + CONV_WIDTH
D_IN_PACKED = C_A + A_PAD

ROW_TILE = 512
PAIR = 2 * CHUNK
VMEM_LIMIT_BYTES = 56 * 1024 * 1024

_BF16 = jnp.bfloat16
_F32 = jnp.float32


def _dot(a, b):
    return jnp.dot(a, b, preferred_element_type=_F32)


def _rms_scale(v):
    return lax.rsqrt(jnp.mean(v * v, axis=-1, keepdims=True) + EPS)


def _layer_kernel(x_ref, p_ref, nw_ref, w1_ref, wup_ref, ba_ref, gnw_ref, cw_ref, cbias_ref,
                  wout_ref, wple_ref, wg_ref, bg_ref, fnw_ref,
                  o_ref, state_ref, carry_ref):
    tm = x_ref.shape[0]
    n_chunks = tm // CHUNK

    @pl.when(pl.program_id(0) == 0)
    def _():
        state_ref[...] = jnp.zeros_like(state_ref)
        carry_ref[...] = jnp.zeros_like(carry_ref)

    x = x_ref[...]
    n = (x * _rms_scale(x) * nw_ref[...]).astype(_BF16)

    def proj(c0, width):
        return _dot(n, w1_ref[:, c0:c0 + width])

    a_low = proj(C_A, A_PAD)
    pre = _dot(a_low.astype(_BF16), wup_ref[...]) + ba_ref[...]
    log_a = (jnp.minimum(pre, 0.0) - jnp.log1p(jnp.exp(-jnp.abs(pre)))) * (1.0 / GATE_TAU)

    row_in_chunk = lax.broadcasted_iota(jnp.int32, (tm, GLA_QK), 0) & (CHUNK - 1)
    cum = log_a
    shift = 1
    while shift < CHUNK:
        cum = cum + jnp.where(row_in_chunk >= shift, pltpu.roll(cum, shift, axis=0), 0.0)
        shift *= 2

    cum3 = cum.reshape(n_chunks, CHUNK, GLA_QK)
    cum_end = cum3[:, CHUNK - 1:CHUNK, :]
    k3 = proj(C_K, GLA_QK).reshape(n_chunks, CHUNK, GLA_QK)
    k_dec = k3 * jnp.exp(cum_end - cum3)
    gamma = jnp.exp(cum_end)

    q = proj(C_Q, GLA_QK).astype(_BF16)
    v = proj(C_V, GLA_WIDTH).astype(_BF16)

    lane = lax.broadcasted_iota(jnp.int32, (GLA_QK, PAIR), 1)
    zeros_blk = jnp.zeros((GLA_DK, GLA_DV), _BF16)
    states = [state_ref[h] for h in range(GLA_HEADS)]
    o_rows = []
    for pair in range(n_chunks // 2):
        r0 = pair * PAIR
        kd_t = k_dec[2 * pair:2 * pair + 2].reshape(PAIR, GLA_QK).T
        kd_even = jnp.where(lane < CHUNK, kd_t, 0.0)
        kd_odd = jnp.where(lane >= CHUNK, kd_t, 0.0)
        d_state = []
        for h in range(GLA_HEADS):
            rows = slice(h * GLA_DK, (h + 1) * GLA_DK)
            lhs = jnp.concatenate([kd_even[rows], kd_odd[rows]], axis=0).astype(_BF16)
            d_state.append(_dot(lhs, v[r0:r0 + PAIR, h * GLA_DV:(h + 1) * GLA_DV]))
        for half in range(2):
            c = 2 * pair + half
            g_col = jnp.broadcast_to(gamma[c], (LANES, GLA_QK)).T
            for h in range(GLA_HEADS):
                rows = slice(h * GLA_DK, (h + 1) * GLA_DK)
                states[h] = (g_col[rows] * states[h]
                             + d_state[h][half * GLA_DK:(half + 1) * GLA_DK])
            sb = [s.astype(_BF16) for s in states]
            o_pairs = []
            for j in range(GLA_HEADS // 2):
                s_bd = jnp.concatenate(
                    [jnp.concatenate([sb[2 * j], zeros_blk], axis=1),
                     jnp.concatenate([zeros_blk, sb[2 * j + 1]], axis=1)], axis=0)
                qc = q[c * CHUNK:(c + 1) * CHUNK, 2 * j * GLA_DK:(2 * j + 2) * GLA_DK]
                o_pairs.append(_dot(qc, s_bd))
            o_rows.append(jnp.concatenate(o_pairs, axis=1))
    for h in range(GLA_HEADS):
        state_ref[h] = states[h]
    o_gla = jnp.concatenate(o_rows, axis=0)

    gnw = gnw_ref[...]
    o_norm = jnp.concatenate(
        [o_gla[:, h * GLA_DV:(h + 1) * GLA_DV] * _rms_scale(o_gla[:, h * GLA_DV:(h + 1) * GLA_DV]) * gnw
         for h in range(GLA_HEADS)], axis=1)
    g_gla = proj(C_G, GLA_WIDTH)
    y_gla = o_norm * (g_gla * jax.nn.sigmoid(g_gla))

    u = proj(C_CC, CONV_WIDTH) * proj(C_CU, CONV_WIDTH)
    prev = carry_ref[...]
    carry_ref[...] = u[tm - SUBLANES:tm]
    row8 = lax.broadcasted_iota(jnp.int32, (SUBLANES, CONV_WIDTH), 0)

    def shifted(s):
        rolled = pltpu.roll(u, s, axis=0)
        head = jnp.where(row8 < s, pltpu.roll(prev, s, axis=0), rolled[0:SUBLANES])
        return jnp.concatenate([head, rolled[SUBLANES:]], axis=0)

    cw = cw_ref[...]
    conv = cbias_ref[...] + cw[0:1] * shifted(2) + cw[1:2] * shifted(1) + cw[2:3] * u
    g_conv = proj(C_GC, CONV_WIDTH)
    y_conv = proj(C_CB, CONV_WIDTH) * conv * (g_conv * jax.nn.sigmoid(g_conv))

    y = jnp.concatenate([y_gla, y_conv], axis=1).astype(_BF16)
    h1 = x + _dot(y, wout_ref[...])
    gate = jax.nn.sigmoid(_dot(h1.astype(_BF16), wg_ref[...]) + bg_ref[...])
    h2 = h1 + gate * _dot(p_ref[...].astype(_BF16), wple_ref[...])
    o_ref[...] = h2 * _rms_scale(h2) * fnw_ref[...]


def _pack_w_in(w_in):
    q0, k0, v0, g0, a0 = 0, GLA_QK, 2 * GLA_QK, 2 * GLA_QK + GLA_WIDTH, 2 * GLA_QK + 2 * GLA_WIDTH
    cb0 = a0 + GATE_RANK
    parts = [
        w_in[:, q0:k0] * (GLA_DK ** -0.5),
        w_in[:, k0:v0],
        w_in[:, v0:g0],
        w_in[:, g0:a0],
        w_in[:, cb0:cb0 + 4 * CONV_WIDTH],
        w_in[:, a0:cb0],
        jnp.zeros((D_MODEL, A_PAD - GATE_RANK), w_in.dtype),
    ]
    return jnp.concatenate(parts, axis=1).astype(_BF16)


def _const_spec(shape):
    return pl.BlockSpec(shape, lambda i: (0,) * len(shape), pipeline_mode=pl.Buffered(1))


@jax.jit
def kernel(x, p, norm_w, w_in, w_alpha_up, b_alpha, gla_norm_w, conv_w, conv_b, w_out, w_ple,
           w_ple_gate, b_ple_gate, final_norm_w):
    b, s, d = x.shape
    assert (b, d) == (1, D_MODEL) and s % ROW_TILE == 0 and norm_w.shape[0] == 1
    h = x.reshape(s, d)
    w1 = _pack_w_in(w_in[0])
    wup = jnp.concatenate(
        [w_alpha_up[0], jnp.zeros((A_PAD - GATE_RANK, GLA_QK), w_alpha_up.dtype)], axis=0).astype(_BF16)

    row_spec = lambda width: pl.BlockSpec((ROW_TILE, width), lambda i: (i, 0))
    out = pl.pallas_call(
        _layer_kernel,
        grid=(s // ROW_TILE,),
        in_specs=[
            row_spec(D_MODEL),
            row_spec(PLE_DIM),
            _const_spec((1, D_MODEL)),
            _const_spec((D_MODEL, D_IN_PACKED)),
            _const_spec((A_PAD, GLA_QK)),
            _const_spec((1, GLA_QK)),
            _const_spec((1, GLA_DV)),
            _const_spec((CONV_K, CONV_WIDTH)),
            _const_spec((1, CONV_WIDTH)),
            _const_spec((D_MIX, D_MODEL)),
            _const_spec((PLE_DIM, D_MODEL)),
            _const_spec((D_MODEL, D_MODEL)),
            _const_spec((1, D_MODEL)),
            _const_spec((1, D_MODEL)),
        ],
        out_specs=row_spec(D_MODEL),
        out_shape=jax.ShapeDtypeStruct((s, d), x.dtype),
        scratch_shapes=[
            pltpu.VMEM((GLA_HEADS, GLA_DK, GLA_DV), _F32),
            pltpu.VMEM((SUBLANES, CONV_WIDTH), _F32),
        ],
        compiler_params=pltpu.CompilerParams(
            dimension_semantics=("arbitrary",),
            vmem_limit_bytes=VMEM_LIMIT_BYTES,
        ),
        name="hybrid_gla_shortconv_layer",
    )(
        h, p.reshape(s, PLE_DIM), norm_w, w1, wup, b_alpha, gla_norm_w, conv_w[0], conv_b,
        w_out[0].astype(_BF16), w_ple[0].astype(_BF16), w_ple_gate[0].astype(_BF16),
        b_ple_gate, final_norm_w.reshape(1, D_MODEL),
    )
    return out.reshape(b, s, d)
```

```python
import jax
import jax.numpy as jnp
from jax import lax
from jax.experimental import pallas as pl
from jax.experimental.pallas import tpu as pltpu

D_MODEL = 1024
CHUNK = 64
PLE_DIM = 256
GLA_HEADS = 4
GLA_DK = 64
GLA_DV = 128
GLA_WIDTH = GLA_HEADS * GLA_DV
GLA_QK = GLA_HEADS * GLA_DK
GATE_RANK = 16
GATE_TAU = 16.0
CONV_WIDTH = 512
CONV_K = 3
D_MIX = GLA_WIDTH + CONV_WIDTH
EPS = 1e-6

LANES = 128
SUBLANES = 8
A_PAD = LANES

C_Q = 0
C_K = C_Q + GLA_QK
C_V = C_K + GLA_QK
C_G = C_V + GLA_WIDTH
C_CB = C_G + GLA_WIDTH
C_CC = C_CB + CONV_WIDTH
C_CU = C_CC + CONV_WIDTH
C_GC = C_CU + CONV_WIDTH
C_A = C_GC + CONV_WIDTH
D_IN_PACKED = C_A + A_PAD

ROW_TILE = 512
PAIR = 2 * CHUNK
VMEM_LIMIT_BYTES = 56 * 1024 * 1024

_BF16 = jnp.bfloat16
_F32 = jnp.float32


def _dot(a, b):
    return jnp.dot(a, b, preferred_element_type=_F32)


def _rms_scale(v):
    return lax.rsqrt(jnp.mean(v * v, axis=-1, keepdims=True) + EPS)


def _layer_kernel(x_ref, p_ref, nw_ref, w1_ref, wup_ref, ba_ref, gnw_ref, cw_ref, cbias_ref,
                  wout_ref, wple_ref, wg_ref, bg_ref, fnw_ref,
                  o_ref, state_ref, carry_ref):
    tm = x_ref.shape[0]
    n_chunks = tm // CHUNK

    @pl.when(pl.program_id(0) == 0)
    def _():
        state_ref[...] = jnp.zeros_like(state_ref)
        carry_ref[...] = jnp.zeros_like(carry_ref)

    ple = _dot(p_ref[...].astype(_BF16), wple_ref[...])

    x = x_ref[...]
    n = (x * _rms_scale(x) * nw_ref[...]).astype(_BF16)

    def proj(c0, width):
        return _dot(n, w1_ref[:, c0:c0 + width])

    a_low = proj(C_A, A_PAD)
    k = proj(C_K, GLA_QK)
    pre = _dot(a_low.astype(_BF16), wup_ref[...]) + ba_ref[...]
    v = proj(C_V, GLA_WIDTH).astype(_BF16)
    q = proj(C_Q, GLA_QK).astype(_BF16)
    u = proj(C_CC, CONV_WIDTH) * proj(C_CU, CONV_WIDTH)

    log_a = (jnp.minimum(pre, 0.0) - jnp.log1p(jnp.exp(-jnp.abs(pre)))) * (1.0 / GATE_TAU)

    row_in_chunk = lax.broadcasted_iota(jnp.int32, (tm, GLA_QK), 0) & (CHUNK - 1)
    cum = log_a
    shift = 1
    while shift < CHUNK:
        cum = cum + jnp.where(row_in_chunk >= shift, pltpu.roll(cum, shift, axis=0), 0.0)
        shift *= 2

    cum3 = cum.reshape(n_chunks, CHUNK, GLA_QK)
    cum_end = cum3[:, CHUNK - 1:CHUNK, :]
    k_dec = k.reshape(n_chunks, CHUNK, GLA_QK) * jnp.exp(cum_end - cum3)
    gamma = jnp.exp(cum_end)

    prev = carry_ref[...]
    carry_ref[...] = u[tm - SUBLANES:tm]
    row8 = lax.broadcasted_iota(jnp.int32, (SUBLANES, CONV_WIDTH), 0)

    def shifted(s):
        rolled = pltpu.roll(u, s, axis=0)
        head = jnp.where(row8 < s, pltpu.roll(prev, s, axis=0), rolled[0:SUBLANES])
        return jnp.concatenate([head, rolled[SUBLANES:]], axis=0)

    cw = cw_ref[...]
    conv = cbias_ref[...] + cw[0:1] * shifted(2) + cw[1:2] * shifted(1) + cw[2:3] * u

    fillers = {0: (C_GC, CONV_WIDTH), 1: (C_CB, CONV_WIDTH), 2: (C_G, GLA_WIDTH)}
    filled = {}

    lane = lax.broadcasted_iota(jnp.int32, (GLA_QK, PAIR), 1)
    zeros_blk = jnp.zeros((GLA_DK, GLA_DV), _BF16)
    states = [state_ref[h] for h in range(GLA_HEADS)]
    o_rows = []
    for pair in range(n_chunks // 2):
        r0 = pair * PAIR
        kd_t = k_dec[2 * pair:2 * pair + 2].reshape(PAIR, GLA_QK).T
        kd_even = jnp.where(lane < CHUNK, kd_t, 0.0)
        kd_odd = jnp.where(lane >= CHUNK, kd_t, 0.0)
        d_state = []
        for h in range(GLA_HEADS):
            rows = slice(h * GLA_DK, (h + 1) * GLA_DK)
            lhs = jnp.concatenate([kd_even[rows], kd_odd[rows]], axis=0).astype(_BF16)
            d_state.append(_dot(lhs, v[r0:r0 + PAIR, h * GLA_DV:(h + 1) * GLA_DV]))
        if pair in fillers:
            filled[fillers[pair][0]] = proj(*fillers[pair])
        for half in range(2):
            c = 2 * pair + half
            g_col = jnp.broadcast_to(gamma[c], (LANES, GLA_QK)).T
            for h in range(GLA_HEADS):
                rows = slice(h * GLA_DK, (h + 1) * GLA_DK)
                states[h] = (g_col[rows] * states[h]
                             + d_state[h][half * GLA_DK:(half + 1) * GLA_DK])
            sb = [s.astype(_BF16) for s in states]
            o_pairs = []
            for j in range(GLA_HEADS // 2):
                s_bd = jnp.concatenate(
                    [jnp.concatenate([sb[2 * j], zeros_blk], axis=1),
                     jnp.concatenate([zeros_blk, sb[2 * j + 1]], axis=1)], axis=0)
                qc = q[c * CHUNK:(c + 1) * CHUNK, 2 * j * GLA_DK:(2 * j + 2) * GLA_DK]
                o_pairs.append(_dot(qc, s_bd))
            o_rows.append(jnp.concatenate(o_pairs, axis=1))
    for h in range(GLA_HEADS):
        state_ref[h] = states[h]
    for c0, width in fillers.values():
        if c0 not in filled:
            filled[c0] = proj(c0, width)
    o_gla = jnp.concatenate(o_rows, axis=0)

    gnw = gnw_ref[...]
    o_norm = jnp.concatenate(
        [o_gla[:, h * GLA_DV:(h + 1) * GLA_DV] * _rms_scale(o_gla[:, h * GLA_DV:(h + 1) * GLA_DV]) * gnw
         for h in range(GLA_HEADS)], axis=1)
    g_gla, g_conv = filled[C_G], filled[C_GC]
    y_gla = o_norm * (g_gla * jax.nn.sigmoid(g_gla))
    y_conv = filled[C_CB] * conv * (g_conv * jax.nn.sigmoid(g_conv))

    y = jnp.concatenate([y_gla, y_conv], axis=1).astype(_BF16)
    halves = [slice(i * (tm // 2), (i + 1) * (tm // 2)) for i in range(2)]
    h1 = [x[r] + _dot(y[r], wout_ref[...]) for r in halves]
    gate = [jax.nn.sigmoid(_dot(hh.astype(_BF16), wg_ref[...]) + bg_ref[...]) for hh in h1]
    for r, hh, gg in zip(halves, h1, gate):
        h2 = hh + gg * ple[r]
        o_ref[r, :] = h2 * _rms_scale(h2) * fnw_ref[...]


def _pack_w_in(w_in):
    q0, k0, v0, g0, a0 = 0, GLA_QK, 2 * GLA_QK, 2 * GLA_QK + GLA_WIDTH, 2 * GLA_QK + 2 * GLA_WIDTH
    cb0 = a0 + GATE_RANK
    parts = [
        w_in[:, q0:k0] * (GLA_DK ** -0.5),
        w_in[:, k0:v0],
        w_in[:, v0:g0],
        w_in[:, g0:a0],
        w_in[:, cb0:cb0 + 4 * CONV_WIDTH],
        w_in[:, a0:cb0],
        jnp.zeros((D_MODEL, A_PAD - GATE_RANK), w_in.dtype),
    ]
    return jnp.concatenate(parts, axis=1).astype(_BF16)


def _const_spec(shape):
    return pl.BlockSpec(shape, lambda i: (0,) * len(shape), pipeline_mode=pl.Buffered(1))


@jax.jit
def kernel(x, p, norm_w, w_in, w_alpha_up, b_alpha, gla_norm_w, conv_w, conv_b, w_out, w_ple,
           w_ple_gate, b_ple_gate, final_norm_w):
    b, s, d = x.shape
    assert (b, d) == (1, D_MODEL) and s % ROW_TILE == 0 and norm_w.shape[0] == 1
    h = x.reshape(s, d)
    w1 = _pack_w_in(w_in[0])
    wup = jnp.concatenate(
        [w_alpha_up[0], jnp.zeros((A_PAD - GATE_RANK, GLA_QK), w_alpha_up.dtype)], axis=0).astype(_BF16)

    row_spec = lambda width: pl.BlockSpec((ROW_TILE, width), lambda i: (i, 0))
    out = pl.pallas_call(
        _layer_kernel,
        grid=(s // ROW_TILE,),
        in_specs=[
            row_spec(D_MODEL),
            row_spec(PLE_DIM),
            _const_spec((1, D_MODEL)),
            _const_spec((D_MODEL, D_IN_PACKED)),
            _const_spec((A_PAD, GLA_QK)),
            _const_spec((1, GLA_QK)),
            _const_spec((1, GLA_DV)),
            _const_spec((CONV_K, CONV_WIDTH)),
            _const_spec((1, CONV_WIDTH)),
            _const_spec((D_MIX, D_MODEL)),
            _const_spec((PLE_DIM, D_MODEL)),
            _const_spec((D_MODEL, D_MODEL)),
            _const_spec((1, D_MODEL)),
            _const_spec((1, D_MODEL)),
        ],
        out_specs=row_spec(D_MODEL),
        out_shape=jax.ShapeDtypeStruct((s, d), x.dtype),
        scratch_shapes=[
            pltpu.VMEM((GLA_HEADS, GLA_DK, GLA_DV), _F32),
            pltpu.VMEM((SUBLANES, CONV_WIDTH), _F32),
        ],
        compiler_params=pltpu.CompilerParams(
            dimension_semantics=("arbitrary",),
            vmem_limit_bytes=VMEM_LIMIT_BYTES,
        ),
        name="hybrid_gla_shortconv_layer",
    )(
        h, p.reshape(s, PLE_DIM), norm_w, w1, wup, b_alpha, gla_norm_w, conv_w[0], conv_b,
        w_out[0].astype(_BF16), w_ple[0].astype(_BF16), w_ple_gate[0].astype(_BF16),
        b_ple_gate, final_norm_w.reshape(1, D_MODEL),
    )
    return out.reshape(b, s, d)
```

```python
import jax
import jax.numpy as jnp
from jax import lax
from jax.experimental import pallas as pl
from jax.experimental.pallas import tpu as pltpu

D_MODEL = 1024
CHUNK = 64
PLE_DIM = 256
GLA_HEADS = 4
GLA_DK = 64
GLA_DV = 128
GLA_WIDTH = GLA_HEADS * GLA_DV
GLA_QK = GLA_HEADS * GLA_DK
GATE_RANK = 16
GATE_TAU = 16.0
CONV_WIDTH = 512
CONV_K = 3
D_MIX = GLA_WIDTH + CONV_WIDTH
EPS = 1e-6

LANES = 128
SUBLANES = 8
A_PAD = LANES

C_Q = 0
C_K = C_Q + GLA_QK
C_V = C_K + GLA_QK
C_G = C_V + GLA_WIDTH
C_CB = C_G + GLA_WIDTH
C_CC = C_CB + CONV_WIDTH
C_CU = C_CC + CONV_WIDTH
C_GC = C_CU + CONV_WIDTH
C_A = C_GC + CONV_WIDTH
D_IN_PACKED = C_A + A_PAD

ROW_TILE = 512
PACK_STEPS = 8
PAIR = 2 * CHUNK
VMEM_LIMIT_BYTES = 56 * 1024 * 1024

_BF16 = jnp.bfloat16
_F32 = jnp.float32


def _dot(a, b):
    return jnp.dot(a, b, preferred_element_type=_F32)


def _rms_scale(v):
    return lax.rsqrt(jnp.mean(v * v, axis=-1, keepdims=True) + EPS)


def _layer_kernel(x_ref, p_ref, nw_ref, w1_ref, wup_ref, ba_ref, gnw_ref, cw_ref, cbias_ref,
                  wout_ref, wple_ref, wg_ref, bg_ref, fnw_ref,
                  o_ref, state_ref, carry_ref):
    tm = x_ref.shape[0]
    n_chunks = tm // CHUNK

    @pl.when(pl.program_id(0) == 0)
    def _():
        state_ref[...] = jnp.zeros_like(state_ref)
        carry_ref[...] = jnp.zeros_like(carry_ref)

    ple = _dot(p_ref[...].astype(_BF16), wple_ref[...])

    x = x_ref[...]
    n = (x * _rms_scale(x) * nw_ref[...]).astype(_BF16)

    def proj(c0, width):
        return _dot(n, w1_ref[:, c0:c0 + width])

    a_low = proj(C_A, A_PAD)
    k = proj(C_K, GLA_QK)
    pre = _dot(a_low.astype(_BF16), wup_ref[...]) + ba_ref[...]
    v = proj(C_V, GLA_WIDTH).astype(_BF16)
    q = proj(C_Q, GLA_QK).astype(_BF16)
    u = proj(C_CC, CONV_WIDTH) * proj(C_CU, CONV_WIDTH)

    log_a = (jnp.minimum(pre, 0.0) - jnp.log1p(jnp.exp(-jnp.abs(pre)))) * (1.0 / GATE_TAU)

    row_in_chunk = lax.broadcasted_iota(jnp.int32, (tm, GLA_QK), 0) & (CHUNK - 1)
    cum = log_a
    shift = 1
    while shift < CHUNK:
        cum = cum + jnp.where(row_in_chunk >= shift, pltpu.roll(cum, shift, axis=0), 0.0)
        shift *= 2

    cum3 = cum.reshape(n_chunks, CHUNK, GLA_QK)
    cum_end = cum3[:, CHUNK - 1:CHUNK, :]
    k_dec = k.reshape(n_chunks, CHUNK, GLA_QK) * jnp.exp(cum_end - cum3)
    gamma = jnp.exp(cum_end)

    prev = carry_ref[...]
    carry_ref[...] = u[tm - SUBLANES:tm]
    row8 = lax.broadcasted_iota(jnp.int32, (SUBLANES, CONV_WIDTH), 0)

    def shifted(s):
        rolled = pltpu.roll(u, s, axis=0)
        head = jnp.where(row8 < s, pltpu.roll(prev, s, axis=0), rolled[0:SUBLANES])
        return jnp.concatenate([head, rolled[SUBLANES:]], axis=0)

    cw = cw_ref[...]
    conv = cbias_ref[...] + cw[0:1] * shifted(2) + cw[1:2] * shifted(1) + cw[2:3] * u

    fillers = {0: (C_GC, CONV_WIDTH), 1: (C_CB, CONV_WIDTH), 2: (C_G, GLA_WIDTH)}
    filled = {}

    lane = lax.broadcasted_iota(jnp.int32, (GLA_QK, PAIR), 1)
    zeros_blk = jnp.zeros((GLA_DK, GLA_DV), _BF16)
    states = [state_ref[h] for h in range(GLA_HEADS)]
    o_rows = []
    for pair in range(n_chunks // 2):
        r0 = pair * PAIR
        kd_t = k_dec[2 * pair:2 * pair + 2].reshape(PAIR, GLA_QK).T
        kd_even = jnp.where(lane < CHUNK, kd_t, 0.0)
        kd_odd = jnp.where(lane >= CHUNK, kd_t, 0.0)
        d_state = []
        for h in range(GLA_HEADS):
            rows = slice(h * GLA_DK, (h + 1) * GLA_DK)
            lhs = jnp.concatenate([kd_even[rows], kd_odd[rows]], axis=0).astype(_BF16)
            d_state.append(_dot(lhs, v[r0:r0 + PAIR, h * GLA_DV:(h + 1) * GLA_DV]))
        if pair in fillers:
            filled[fillers[pair][0]] = proj(*fillers[pair])
        for half in range(2):
            c = 2 * pair + half
            g_col = jnp.broadcast_to(gamma[c], (LANES, GLA_QK)).T
            for h in range(GLA_HEADS):
                rows = slice(h * GLA_DK, (h + 1) * GLA_DK)
                states[h] = (g_col[rows] * states[h]
                             + d_state[h][half * GLA_DK:(half + 1) * GLA_DK])
            sb = [s.astype(_BF16) for s in states]
            o_pairs = []
            for j in range(GLA_HEADS // 2):
                s_bd = jnp.concatenate(
                    [jnp.concatenate([sb[2 * j], zeros_blk], axis=1),
                     jnp.concatenate([zeros_blk, sb[2 * j + 1]], axis=1)], axis=0)
                qc = q[c * CHUNK:(c + 1) * CHUNK, 2 * j * GLA_DK:(2 * j + 2) * GLA_DK]
                o_pairs.append(_dot(qc, s_bd))
            o_rows.append(jnp.concatenate(o_pairs, axis=1))
    for h in range(GLA_HEADS):
        state_ref[h] = states[h]
    for c0, width in fillers.values():
        if c0 not in filled:
            filled[c0] = proj(c0, width)
    o_gla = jnp.concatenate(o_rows, axis=0)

    gnw = gnw_ref[...]
    o_norm = jnp.concatenate(
        [o_gla[:, h * GLA_DV:(h + 1) * GLA_DV] * _rms_scale(o_gla[:, h * GLA_DV:(h + 1) * GLA_DV]) * gnw
         for h in range(GLA_HEADS)], axis=1)
    g_gla, g_conv = filled[C_G], filled[C_GC]
    y_gla = o_norm * (g_gla * jax.nn.sigmoid(g_gla))
    y_conv = filled[C_CB] * conv * (g_conv * jax.nn.sigmoid(g_conv))

    y = jnp.concatenate([y_gla, y_conv], axis=1).astype(_BF16)
    halves = [slice(i * (tm // 2), (i + 1) * (tm // 2)) for i in range(2)]
    h1 = [x[r] + _dot(y[r], wout_ref[...]) for r in halves]
    gate = [jax.nn.sigmoid(_dot(hh.astype(_BF16), wg_ref[...]) + bg_ref[...]) for hh in h1]
    for r, hh, gg in zip(halves, h1, gate):
        h2 = hh + gg * ple[r]
        o_ref[r, :] = h2 * _rms_scale(h2) * fnw_ref[...]


def _pack_kernel(win_ref, wout_ref, wg_ref, wple_ref, wup_ref, w1_o, wout_o, wg_o, wple_o, wup_o):
    rows = win_ref.shape[0]
    a0 = 2 * GLA_QK + 2 * GLA_WIDTH
    cb0 = a0 + GATE_RANK
    w1_o[:, C_Q:C_K] = (win_ref[:, 0:GLA_QK] * (GLA_DK ** -0.5)).astype(_BF16)
    w1_o[:, C_K:C_CB] = win_ref[:, GLA_QK:a0].astype(_BF16)
    w1_o[:, C_CB:C_A] = win_ref[:, cb0:cb0 + 4 * CONV_WIDTH].astype(_BF16)
    w1_o[:, C_A:D_IN_PACKED] = jnp.concatenate(
        [win_ref[:, a0:cb0], jnp.zeros((rows, A_PAD - GATE_RANK), _F32)], axis=1).astype(_BF16)
    wout_o[...] = wout_ref[...].astype(_BF16)
    wg_o[...] = wg_ref[...].astype(_BF16)
    wple_o[...] = wple_ref[...].astype(_BF16)
    wup_o[...] = jnp.concatenate(
        [wup_ref[...], jnp.zeros((A_PAD - GATE_RANK, GLA_QK), _F32)], axis=0).astype(_BF16)


def _pack_weights(w_in, w_out, w_gate, w_ple, w_up):
    steps = PACK_STEPS
    d_in = w_in.shape[-1]
    row_blk = lambda rows, width: pl.BlockSpec((None, rows // steps, width), lambda i: (0, i, 0))
    out_blk = lambda rows, width: pl.BlockSpec((rows // steps, width), lambda i: (i, 0))
    return pl.pallas_call(
        _pack_kernel,
        grid=(steps,),
        in_specs=[
            row_blk(D_MODEL, d_in),
            row_blk(D_MIX, D_MODEL),
            row_blk(D_MODEL, D_MODEL),
            row_blk(PLE_DIM, D_MODEL),
            pl.BlockSpec((None, GATE_RANK, GLA_QK), lambda i: (0, 0, 0)),
        ],
        out_specs=[
            out_blk(D_MODEL, D_IN_PACKED),
            out_blk(D_MIX, D_MODEL),
            out_blk(D_MODEL, D_MODEL),
            out_blk(PLE_DIM, D_MODEL),
            pl.BlockSpec((A_PAD, GLA_QK), lambda i: (0, 0)),
        ],
        out_shape=[
            jax.ShapeDtypeStruct((D_MODEL, D_IN_PACKED), _BF16),
            jax.ShapeDtypeStruct((D_MIX, D_MODEL), _BF16),
            jax.ShapeDtypeStruct((D_MODEL, D_MODEL), _BF16),
            jax.ShapeDtypeStruct((PLE_DIM, D_MODEL), _BF16),
            jax.ShapeDtypeStruct((A_PAD, GLA_QK), _BF16),
        ],
        compiler_params=pltpu.CompilerParams(dimension_semantics=("arbitrary",)),
        name="pack_weights_bf16",
    )(w_in, w_out, w_gate, w_ple, w_up)


def _const_spec(shape):
    return pl.BlockSpec(shape, lambda i: (0,) * len(shape), pipeline_mode=pl.Buffered(1))


@jax.jit
def kernel(x, p, norm_w, w_in, w_alpha_up, b_alpha, gla_norm_w, conv_w, conv_b, w_out, w_ple,
           w_ple_gate, b_ple_gate, final_norm_w):
    b, s, d = x.shape
    assert (b, d) == (1, D_MODEL) and s % ROW_TILE == 0 and norm_w.shape[0] == 1
    h = x.reshape(s, d)
    w1, wout, wg, wple, wup = _pack_weights(w_in, w_out, w_ple_gate, w_ple, w_alpha_up)

    row_spec = lambda width: pl.BlockSpec((ROW_TILE, width), lambda i: (i, 0))
    out = pl.pallas_call(
        _layer_kernel,
        grid=(s // ROW_TILE,),
        in_specs=[
            row_spec(D_MODEL),
            row_spec(PLE_DIM),
            _const_spec((1, D_MODEL)),
            _const_spec((D_MODEL, D_IN_PACKED)),
            _const_spec((A_PAD, GLA_QK)),
            _const_spec((1, GLA_QK)),
            _const_spec((1, GLA_DV)),
            _const_spec((CONV_K, CONV_WIDTH)),
            _const_spec((1, CONV_WIDTH)),
            _const_spec((D_MIX, D_MODEL)),
            _const_spec((PLE_DIM, D_MODEL)),
            _const_spec((D_MODEL, D_MODEL)),
            _const_spec((1, D_MODEL)),
            _const_spec((1, D_MODEL)),
        ],
        out_specs=row_spec(D_MODEL),
        out_shape=jax.ShapeDtypeStruct((s, d), x.dtype),
        scratch_shapes=[
            pltpu.VMEM((GLA_HEADS, GLA_DK, GLA_DV), _F32),
            pltpu.VMEM((SUBLANES, CONV_WIDTH), _F32),
        ],
        compiler_params=pltpu.CompilerParams(
            dimension_semantics=("arbitrary",),
            vmem_limit_bytes=VMEM_LIMIT_BYTES,
        ),
        name="hybrid_gla_shortconv_layer",
    )(
        h, p.reshape(s, PLE_DIM), norm_w, w1, wup, b_alpha, gla_norm_w, conv_w[0], conv_b,
        wout, wple, wg, b_ple_gate, final_norm_w.reshape(1, D_MODEL),
    )
    return out.reshape(b, s, d)
```

```python
import jax
import jax.numpy as jnp
from jax import lax
from jax.experimental import pallas as pl
from jax.experimental.pallas import tpu as pltpu

D_MODEL = 1024
CHUNK = 64
PLE_DIM = 256
GLA_HEADS = 4
GLA_DK = 64
GLA_DV = 128
GLA_WIDTH = GLA_HEADS * GLA_DV
GLA_QK = GLA_HEADS * GLA_DK
GATE_RANK = 16
GATE_TAU = 16.0
CONV_WIDTH = 512
CONV_K = 3
D_MIX = GLA_WIDTH + CONV_WIDTH
EPS = 1e-6

LANES = 128
SUBLANES = 8
A_PAD = LANES

C_Q = 0
C_K = C_Q + GLA_QK
C_V = C_K + GLA_QK
C_G = C_V + GLA_WIDTH
C_CB = C_G + GLA_WIDTH
C_CC = C_CB + CONV_WIDTH
C_CU = C_CC + CONV_WIDTH
C_GC = C_CU + CONV_WIDTH
C_A = C_GC + CONV_WIDTH
D_IN_PACKED = C_A + A_PAD

ROW_TILE = 512
PACK_STEPS = 8
PAIR = 2 * CHUNK
VMEM_LIMIT_BYTES = 56 * 1024 * 1024

_BF16 = jnp.bfloat16
_F32 = jnp.float32


def _dot(a, b):
    return jnp.dot(a, b, preferred_element_type=_F32)


def _rms_scale(v):
    return lax.rsqrt(jnp.mean(v * v, axis=-1, keepdims=True) + EPS)


def _layer_kernel(x_ref, p_ref, nw_ref, w1t_ref, wup_ref, ba_ref, gnw_ref, cw_ref, cbias_ref,
                  wout_ref, wple_ref, wg_ref, bg_ref, fnw_ref,
                  o_ref, state_ref, carry_ref):
    tm = x_ref.shape[0]
    n_chunks = tm // CHUNK

    @pl.when(pl.program_id(0) == 0)
    def _():
        state_ref[...] = jnp.zeros_like(state_ref)
        carry_ref[...] = jnp.zeros_like(carry_ref)

    ple = _dot(p_ref[...].astype(_BF16), wple_ref[...])

    x = x_ref[...]
    n = (x * _rms_scale(x) * nw_ref[...]).astype(_BF16)

    def proj(c0, width):
        return lax.dot_general(n, w1t_ref[c0:c0 + width, :], (((1,), (1,)), ((), ())),
                               preferred_element_type=_F32)

    a_low = proj(C_A, A_PAD)
    k = proj(C_K, GLA_QK)
    pre = _dot(a_low.astype(_BF16), wup_ref[...]) + ba_ref[...]
    v = proj(C_V, GLA_WIDTH).astype(_BF16)
    q = proj(C_Q, GLA_QK).astype(_BF16)
    u = proj(C_CC, CONV_WIDTH) * proj(C_CU, CONV_WIDTH)

    log_a = (jnp.minimum(pre, 0.0) - jnp.log1p(jnp.exp(-jnp.abs(pre)))) * (1.0 / GATE_TAU)

    row_in_chunk = lax.broadcasted_iota(jnp.int32, (tm, GLA_QK), 0) & (CHUNK - 1)
    cum = log_a
    shift = 1
    while shift < CHUNK:
        cum = cum + jnp.where(row_in_chunk >= shift, pltpu.roll(cum, shift, axis=0), 0.0)
        shift *= 2

    cum3 = cum.reshape(n_chunks, CHUNK, GLA_QK)
    cum_end = cum3[:, CHUNK - 1:CHUNK, :]
    k_dec = k.reshape(n_chunks, CHUNK, GLA_QK) * jnp.exp(cum_end - cum3)
    gamma = jnp.exp(cum_end)

    prev = carry_ref[...]
    carry_ref[...] = u[tm - SUBLANES:tm]
    row8 = lax.broadcasted_iota(jnp.int32, (SUBLANES, CONV_WIDTH), 0)

    def shifted(s):
        rolled = pltpu.roll(u, s, axis=0)
        head = jnp.where(row8 < s, pltpu.roll(prev, s, axis=0), rolled[0:SUBLANES])
        return jnp.concatenate([head, rolled[SUBLANES:]], axis=0)

    cw = cw_ref[...]
    conv = cbias_ref[...] + cw[0:1] * shifted(2) + cw[1:2] * shifted(1) + cw[2:3] * u

    fillers = {0: (C_GC, CONV_WIDTH), 1: (C_CB, CONV_WIDTH), 2: (C_G, GLA_WIDTH)}
    filled = {}

    lane = lax.broadcasted_iota(jnp.int32, (GLA_QK, PAIR), 1)
    zeros_blk = jnp.zeros((GLA_DK, GLA_DV), _BF16)
    states = [state_ref[h] for h in range(GLA_HEADS)]
    o_rows = []
    for pair in range(n_chunks // 2):
        r0 = pair * PAIR
        kd_t = k_dec[2 * pair:2 * pair + 2].reshape(PAIR, GLA_QK).T
        kd_even = jnp.where(lane < CHUNK, kd_t, 0.0)
        kd_odd = jnp.where(lane >= CHUNK, kd_t, 0.0)
        d_state = []
        for h in range(GLA_HEADS):
            rows = slice(h * GLA_DK, (h + 1) * GLA_DK)
            lhs = jnp.concatenate([kd_even[rows], kd_odd[rows]], axis=0).astype(_BF16)
            d_state.append(_dot(lhs, v[r0:r0 + PAIR, h * GLA_DV:(h + 1) * GLA_DV]))
        if pair in fillers:
            filled[fillers[pair][0]] = proj(*fillers[pair])
        for half in range(2):
            c = 2 * pair + half
            g_col = jnp.broadcast_to(gamma[c], (LANES, GLA_QK)).T
            for h in range(GLA_HEADS):
                rows = slice(h * GLA_DK, (h + 1) * GLA_DK)
                states[h] = (g_col[rows] * states[h]
                             + d_state[h][half * GLA_DK:(half + 1) * GLA_DK])
            sb = [s.astype(_BF16) for s in states]
            o_pairs = []
            for j in range(GLA_HEADS // 2):
                s_bd = jnp.concatenate(
                    [jnp.concatenate([sb[2 * j], zeros_blk], axis=1),
                     jnp.concatenate([zeros_blk, sb[2 * j + 1]], axis=1)], axis=0)
                qc = q[c * CHUNK:(c + 1) * CHUNK, 2 * j * GLA_DK:(2 * j + 2) * GLA_DK]
                o_pairs.append(_dot(qc, s_bd))
            o_rows.append(jnp.concatenate(o_pairs, axis=1))
    for h in range(GLA_HEADS):
        state_ref[h] = states[h]
    for c0, width in fillers.values():
        if c0 not in filled:
            filled[c0] = proj(c0, width)
    o_gla = jnp.concatenate(o_rows, axis=0)

    gnw = gnw_ref[...]
    o_norm = jnp.concatenate(
        [o_gla[:, h * GLA_DV:(h + 1) * GLA_DV] * _rms_scale(o_gla[:, h * GLA_DV:(h + 1) * GLA_DV]) * gnw
         for h in range(GLA_HEADS)], axis=1)
    g_gla, g_conv = filled[C_G], filled[C_GC]
    y_gla = o_norm * (g_gla * jax.nn.sigmoid(g_gla))
    y_conv = filled[C_CB] * conv * (g_conv * jax.nn.sigmoid(g_conv))

    y = jnp.concatenate([y_gla, y_conv], axis=1).astype(_BF16)
    halves = [slice(i * (tm // 2), (i + 1) * (tm // 2)) for i in range(2)]
    h1 = [x[r] + _dot(y[r], wout_ref[...]) for r in halves]
    gate = [jax.nn.sigmoid(_dot(hh.astype(_BF16), wg_ref[...]) + bg_ref[...]) for hh in h1]
    for r, hh, gg in zip(halves, h1, gate):
        h2 = hh + gg * ple[r]
        o_ref[r, :] = h2 * _rms_scale(h2) * fnw_ref[...]


def _pack_kernel(wint_ref, wout_ref, wg_ref, wple_ref, wup_ref, w1t_o, wout_o, wg_o, wple_o, wup_o):
    a0 = 2 * GLA_QK + 2 * GLA_WIDTH
    cb0 = a0 + GATE_RANK
    w1t_o[C_Q:C_K, :] = (wint_ref[0:GLA_QK, :] * (GLA_DK ** -0.5)).astype(_BF16)
    w1t_o[C_K:C_CB, :] = wint_ref[GLA_QK:a0, :].astype(_BF16)
    w1t_o[C_CB:C_A, :] = wint_ref[cb0:cb0 + 4 * CONV_WIDTH, :].astype(_BF16)
    w1t_o[C_A:C_A + GATE_RANK, :] = wint_ref[a0:cb0, :].astype(_BF16)
    w1t_o[C_A + GATE_RANK:D_IN_PACKED, :] = jnp.zeros((A_PAD - GATE_RANK, w1t_o.shape[1]), _BF16)
    wout_o[...] = wout_ref[...].astype(_BF16)
    wg_o[...] = wg_ref[...].astype(_BF16)
    wple_o[...] = wple_ref[...].astype(_BF16)
    wup_o[...] = jnp.concatenate(
        [wup_ref[...], jnp.zeros((A_PAD - GATE_RANK, GLA_QK), _F32)], axis=0).astype(_BF16)


def _pack_weights(w_in_t, w_out, w_gate, w_ple, w_up):
    steps = PACK_STEPS
    d_in = w_in_t.shape[1]
    row_blk = lambda rows, width: pl.BlockSpec((None, rows // steps, width), lambda i: (0, i, 0))
    out_blk = lambda rows, width: pl.BlockSpec((rows // steps, width), lambda i: (i, 0))
    return pl.pallas_call(
        _pack_kernel,
        grid=(steps,),
        in_specs=[
            pl.BlockSpec((None, d_in, D_MODEL // steps), lambda i: (0, 0, i)),
            row_blk(D_MIX, D_MODEL),
            row_blk(D_MODEL, D_MODEL),
            row_blk(PLE_DIM, D_MODEL),
            pl.BlockSpec((None, GATE_RANK, GLA_QK), lambda i: (0, 0, 0)),
        ],
        out_specs=[
            pl.BlockSpec((D_IN_PACKED, D_MODEL // steps), lambda i: (0, i)),
            out_blk(D_MIX, D_MODEL),
            out_blk(D_MODEL, D_MODEL),
            out_blk(PLE_DIM, D_MODEL),
            pl.BlockSpec((A_PAD, GLA_QK), lambda i: (0, 0)),
        ],
        out_shape=[
            jax.ShapeDtypeStruct((D_IN_PACKED, D_MODEL), _BF16),
            jax.ShapeDtypeStruct((D_MIX, D_MODEL), _BF16),
            jax.ShapeDtypeStruct((D_MODEL, D_MODEL), _BF16),
            jax.ShapeDtypeStruct((PLE_DIM, D_MODEL), _BF16),
            jax.ShapeDtypeStruct((A_PAD, GLA_QK), _BF16),
        ],
        compiler_params=pltpu.CompilerParams(dimension_semantics=("arbitrary",)),
        name="pack_weights_bf16",
    )(w_in_t, w_out, w_gate, w_ple, w_up)


def _const_spec(shape):
    return pl.BlockSpec(shape, lambda i: (0,) * len(shape), pipeline_mode=pl.Buffered(1))


@jax.jit
def kernel(x, p, norm_w, w_in, w_alpha_up, b_alpha, gla_norm_w, conv_w, conv_b, w_out, w_ple,
           w_ple_gate, b_ple_gate, final_norm_w):
    b, s, d = x.shape
    assert (b, d) == (1, D_MODEL) and s % ROW_TILE == 0 and norm_w.shape[0] == 1
    h = x.reshape(s, d)
    w1t, wout, wg, wple, wup = _pack_weights(
        jnp.swapaxes(w_in, 1, 2), w_out, w_ple_gate, w_ple, w_alpha_up)

    row_spec = lambda width: pl.BlockSpec((ROW_TILE, width), lambda i: (i, 0))
    out = pl.pallas_call(
        _layer_kernel,
        grid=(s // ROW_TILE,),
        in_specs=[
            row_spec(D_MODEL),
            row_spec(PLE_DIM),
            _const_spec((1, D_MODEL)),
            _const_spec((D_IN_PACKED, D_MODEL)),
            _const_spec((A_PAD, GLA_QK)),
            _const_spec((1, GLA_QK)),
            _const_spec((1, GLA_DV)),
            _const_spec((CONV_K, CONV_WIDTH)),
            _const_spec((1, CONV_WIDTH)),
            _const_spec((D_MIX, D_MODEL)),
            _const_spec((PLE_DIM, D_MODEL)),
            _const_spec((D_MODEL, D_MODEL)),
            _const_spec((1, D_MODEL)),
            _const_spec((1, D_MODEL)),
        ],
        out_specs=row_spec(D_MODEL),
        out_shape=jax.ShapeDtypeStruct((s, d), x.dtype),
        scratch_shapes=[
            pltpu.VMEM((GLA_HEADS, GLA_DK, GLA_DV), _F32),
            pltpu.VMEM((SUBLANES, CONV_WIDTH), _F32),
        ],
        compiler_params=pltpu.CompilerParams(
            dimension_semantics=("arbitrary",),
            vmem_limit_bytes=VMEM_LIMIT_BYTES,
        ),
        name="hybrid_gla_shortconv_layer",
    )(
        h, p.reshape(s, PLE_DIM), norm_w, w1t, wup, b_alpha, gla_norm_w, conv_w[0], conv_b,
        wout, wple, wg, b_ple_gate, final_norm_w.reshape(1, D_MODEL),
    )
    return out.reshape(b, s, d)
```

```python
import jax
import jax.numpy as jnp
from jax import lax
from jax.experimental import pallas as pl
from jax.experimental.pallas import tpu as pltpu

D_MODEL = 1024
CHUNK = 64
PLE_DIM = 256
GLA_HEADS = 4
GLA_DK = 64
GLA_DV = 128
GLA_WIDTH = GLA_HEADS * GLA_DV
GLA_QK = GLA_HEADS * GLA_DK
GATE_RANK = 16
GATE_TAU = 16.0
CONV_WIDTH = 512
CONV_K = 3
D_MIX = GLA_WIDTH + CONV_WIDTH
EPS = 1e-6

LANES = 128
SUBLANES = 8
A_PAD = LANES

C_Q = 0
C_K = C_Q + GLA_QK
C_V = C_K + GLA_QK
C_G = C_V + GLA_WIDTH
C_CB = C_G + GLA_WIDTH
C_CC = C_CB + CONV_WIDTH
C_CU = C_CC + CONV_WIDTH
C_GC = C_CU + CONV_WIDTH
C_A = C_GC + CONV_WIDTH
D_IN_PACKED = C_A + A_PAD

ROW_TILE = 512
PACK_STEPS = 8
PAIR = 2 * CHUNK
VMEM_LIMIT_BYTES = 56 * 1024 * 1024

_BF16 = jnp.bfloat16
_F32 = jnp.float32


def _dot(a, b):
    return jnp.dot(a, b, preferred_element_type=_F32)


def _rms_scale(v):
    return lax.rsqrt(jnp.mean(v * v, axis=-1, keepdims=True) + EPS)


def _layer_kernel(x_ref, p_ref, nw_ref, w1_ref, wup_ref, ba_ref, gnw_ref, cw_ref, cbias_ref,
                  wout_ref, wple_ref, wg_ref, bg_ref, fnw_ref,
                  o_ref, state_ref, carry_ref):
    tm = x_ref.shape[0]
    n_chunks = tm // CHUNK

    @pl.when(pl.program_id(0) == 0)
    def _():
        state_ref[...] = jnp.zeros_like(state_ref)
        carry_ref[...] = jnp.zeros_like(carry_ref)

    ple = _dot(p_ref[...].astype(_BF16), wple_ref[...])

    x = x_ref[...]
    n = (x * _rms_scale(x) * nw_ref[...]).astype(_BF16)

    def proj(c0, width):
        return _dot(n, w1_ref[:, c0:c0 + width])

    a_low = proj(C_A, A_PAD)
    k = proj(C_K, GLA_QK)
    pre = _dot(a_low.astype(_BF16), wup_ref[...]) + ba_ref[...]
    v = proj(C_V, GLA_WIDTH).astype(_BF16)
    q = proj(C_Q, GLA_QK).astype(_BF16)
    u = proj(C_CC, CONV_WIDTH) * proj(C_CU, CONV_WIDTH)

    log_a = (jnp.minimum(pre, 0.0) - jnp.log1p(jnp.exp(-jnp.abs(pre)))) * (1.0 / GATE_TAU)

    row_in_chunk = lax.broadcasted_iota(jnp.int32, (tm, GLA_QK), 0) & (CHUNK - 1)
    cum = log_a
    shift = 1
    while shift < CHUNK:
        cum = cum + jnp.where(row_in_chunk >= shift, pltpu.roll(cum, shift, axis=0), 0.0)
        shift *= 2

    cum3 = cum.reshape(n_chunks, CHUNK, GLA_QK)
    cum_end = cum3[:, CHUNK - 1:CHUNK, :]
    k_dec = k.reshape(n_chunks, CHUNK, GLA_QK) * jnp.exp(cum_end - cum3)
    gamma = jnp.exp(cum_end)

    prev = carry_ref[...]
    carry_ref[...] = u[tm - SUBLANES:tm]
    row8 = lax.broadcasted_iota(jnp.int32, (SUBLANES, CONV_WIDTH), 0)

    def shifted(s):
        rolled = pltpu.roll(u, s, axis=0)
        head = jnp.where(row8 < s, pltpu.roll(prev, s, axis=0), rolled[0:SUBLANES])
        return jnp.concatenate([head, rolled[SUBLANES:]], axis=0)

    cw = cw_ref[...]
    conv = cbias_ref[...] + cw[0:1] * shifted(2) + cw[1:2] * shifted(1) + cw[2:3] * u

    fillers = {0: (C_GC, CONV_WIDTH), 1: (C_CB, CONV_WIDTH), 2: (C_G, GLA_WIDTH)}
    filled = {}

    lane = lax.broadcasted_iota(jnp.int32, (GLA_QK, PAIR), 1)
    zeros_blk = jnp.zeros((GLA_DK, GLA_DV), _BF16)
    states = [state_ref[h] for h in range(GLA_HEADS)]
    o_rows = []
    for pair in range(n_chunks // 2):
        r0 = pair * PAIR
        kd_t = k_dec[2 * pair:2 * pair + 2].reshape(PAIR, GLA_QK).T
        kd_even = jnp.where(lane < CHUNK, kd_t, 0.0)
        kd_odd = jnp.where(lane >= CHUNK, kd_t, 0.0)
        d_state = []
        for h in range(GLA_HEADS):
            rows = slice(h * GLA_DK, (h + 1) * GLA_DK)
            lhs = jnp.concatenate([kd_even[rows], kd_odd[rows]], axis=0).astype(_BF16)
            d_state.append(_dot(lhs, v[r0:r0 + PAIR, h * GLA_DV:(h + 1) * GLA_DV]))
        if pair in fillers:
            filled[fillers[pair][0]] = proj(*fillers[pair])
        for half in range(2):
            c = 2 * pair + half
            g_col = jnp.broadcast_to(gamma[c], (LANES, GLA_QK)).T
            for h in range(GLA_HEADS):
                rows = slice(h * GLA_DK, (h + 1) * GLA_DK)
                states[h] = (g_col[rows] * states[h]
                             + d_state[h][half * GLA_DK:(half + 1) * GLA_DK])
            sb = [s.astype(_BF16) for s in states]
            o_pairs = []
            for j in range(GLA_HEADS // 2):
                s_bd = jnp.concatenate(
                    [jnp.concatenate([sb[2 * j], zeros_blk], axis=1),
                     jnp.concatenate([zeros_blk, sb[2 * j + 1]], axis=1)], axis=0)
                qc = q[c * CHUNK:(c + 1) * CHUNK, 2 * j * GLA_DK:(2 * j + 2) * GLA_DK]
                o_pairs.append(_dot(qc, s_bd))
            o_rows.append(jnp.concatenate(o_pairs, axis=1))
    for h in range(GLA_HEADS):
        state_ref[h] = states[h]
    for c0, width in fillers.values():
        if c0 not in filled:
            filled[c0] = proj(c0, width)
    o_gla = jnp.concatenate(o_rows, axis=0)

    gnw = gnw_ref[...]
    o_norm = jnp.concatenate(
        [o_gla[:, h * GLA_DV:(h + 1) * GLA_DV] * _rms_scale(o_gla[:, h * GLA_DV:(h + 1) * GLA_DV]) * gnw
         for h in range(GLA_HEADS)], axis=1)
    g_gla, g_conv = filled[C_G], filled[C_GC]
    y_gla = o_norm * (g_gla * jax.nn.sigmoid(g_gla))
    y_conv = filled[C_CB] * conv * (g_conv * jax.nn.sigmoid(g_conv))

    y = jnp.concatenate([y_gla, y_conv], axis=1).astype(_BF16)
    halves = [slice(i * (tm // 2), (i + 1) * (tm // 2)) for i in range(2)]
    h1 = [x[r] + _dot(y[r], wout_ref[...]) for r in halves]
    gate = [jax.nn.sigmoid(_dot(hh.astype(_BF16), wg_ref[...]) + bg_ref[...]) for hh in h1]
    for r, hh, gg in zip(halves, h1, gate):
        h2 = hh + gg * ple[r]
        o_ref[r, :] = h2 * _rms_scale(h2) * fnw_ref[...]


def _pack_kernel(wint_ref, wout_ref, wg_ref, wple_ref, wup_ref, w1_o, wout_o, wg_o, wple_o, wup_o):
    a0 = 2 * GLA_QK + 2 * GLA_WIDTH
    cb0 = a0 + GATE_RANK
    blk = wint_ref.shape[1]
    eye = (lax.broadcasted_iota(jnp.int32, (blk, blk), 0)
           == lax.broadcasted_iota(jnp.int32, (blk, blk), 1)).astype(_BF16)

    def transposed(rows_bf16):
        out = lax.dot_general(eye, rows_bf16, (((1,), (1,)), ((), ())), preferred_element_type=_F32)
        return out.astype(_BF16)

    w1_o[:, C_Q:C_K] = transposed((wint_ref[0:GLA_QK, :] * (GLA_DK ** -0.5)).astype(_BF16))
    w1_o[:, C_K:C_CB] = transposed(wint_ref[GLA_QK:a0, :].astype(_BF16))
    w1_o[:, C_CB:C_A] = transposed(wint_ref[cb0:cb0 + 4 * CONV_WIDTH, :].astype(_BF16))
    a_rows = jnp.concatenate(
        [wint_ref[a0:cb0, :], jnp.zeros((A_PAD - GATE_RANK, blk), _F32)], axis=0).astype(_BF16)
    w1_o[:, C_A:D_IN_PACKED] = transposed(a_rows)
    wout_o[...] = wout_ref[...].astype(_BF16)
    wg_o[...] = wg_ref[...].astype(_BF16)
    wple_o[...] = wple_ref[...].astype(_BF16)
    wup_o[...] = jnp.concatenate(
        [wup_ref[...], jnp.zeros((A_PAD - GATE_RANK, GLA_QK), _F32)], axis=0).astype(_BF16)


def _pack_weights(w_in_t, w_out, w_gate, w_ple, w_up):
    steps = PACK_STEPS
    d_in = w_in_t.shape[1]
    row_blk = lambda rows, width: pl.BlockSpec((None, rows // steps, width), lambda i: (0, i, 0))
    out_blk = lambda rows, width: pl.BlockSpec((rows // steps, width), lambda i: (i, 0))
    return pl.pallas_call(
        _pack_kernel,
        grid=(steps,),
        in_specs=[
            pl.BlockSpec((None, d_in, D_MODEL // steps), lambda i: (0, 0, i)),
            row_blk(D_MIX, D_MODEL),
            row_blk(D_MODEL, D_MODEL),
            row_blk(PLE_DIM, D_MODEL),
            pl.BlockSpec((None, GATE_RANK, GLA_QK), lambda i: (0, 0, 0)),
        ],
        out_specs=[
            out_blk(D_MODEL, D_IN_PACKED),
            out_blk(D_MIX, D_MODEL),
            out_blk(D_MODEL, D_MODEL),
            out_blk(PLE_DIM, D_MODEL),
            pl.BlockSpec((A_PAD, GLA_QK), lambda i: (0, 0)),
        ],
        out_shape=[
            jax.ShapeDtypeStruct((D_MODEL, D_IN_PACKED), _BF16),
            jax.ShapeDtypeStruct((D_MIX, D_MODEL), _BF16),
            jax.ShapeDtypeStruct((D_MODEL, D_MODEL), _BF16),
            jax.ShapeDtypeStruct((PLE_DIM, D_MODEL), _BF16),
            jax.ShapeDtypeStruct((A_PAD, GLA_QK), _BF16),
        ],
        compiler_params=pltpu.CompilerParams(dimension_semantics=("arbitrary",)),
        name="pack_weights_bf16",
    )(w_in_t, w_out, w_gate, w_ple, w_up)


def _const_spec(shape):
    return pl.BlockSpec(shape, lambda i: (0,) * len(shape), pipeline_mode=pl.Buffered(1))


@jax.jit
def kernel(x, p, norm_w, w_in, w_alpha_up, b_alpha, gla_norm_w, conv_w, conv_b, w_out, w_ple,
           w_ple_gate, b_ple_gate, final_norm_w):
    b, s, d = x.shape
    assert (b, d) == (1, D_MODEL) and s % ROW_TILE == 0 and norm_w.shape[0] == 1
    h = x.reshape(s, d)
    w1, wout, wg, wple, wup = _pack_weights(
        jnp.swapaxes(w_in, 1, 2), w_out, w_ple_gate, w_ple, w_alpha_up)

    row_spec = lambda width: pl.BlockSpec((ROW_TILE, width), lambda i: (i, 0))
    out = pl.pallas_call(
        _layer_kernel,
        grid=(s // ROW_TILE,),
        in_specs=[
            row_spec(D_MODEL),
            row_spec(PLE_DIM),
            _const_spec((1, D_MODEL)),
            _const_spec((D_MODEL, D_IN_PACKED)),
            _const_spec((A_PAD, GLA_QK)),
            _const_spec((1, GLA_QK)),
            _const_spec((1, GLA_DV)),
            _const_spec((CONV_K, CONV_WIDTH)),
            _const_spec((1, CONV_WIDTH)),
            _const_spec((D_MIX, D_MODEL)),
            _const_spec((PLE_DIM, D_MODEL)),
            _const_spec((D_MODEL, D_MODEL)),
            _const_spec((1, D_MODEL)),
            _const_spec((1, D_MODEL)),
        ],
        out_specs=row_spec(D_MODEL),
        out_shape=jax.ShapeDtypeStruct((s, d), x.dtype),
        scratch_shapes=[
            pltpu.VMEM((GLA_HEADS, GLA_DK, GLA_DV), _F32),
            pltpu.VMEM((SUBLANES, CONV_WIDTH), _F32),
        ],
        compiler_params=pltpu.CompilerParams(
            dimension_semantics=("arbitrary",),
            vmem_limit_bytes=VMEM_LIMIT_BYTES,
        ),
        name="hybrid_gla_shortconv_layer",
    )(
        h, p.reshape(s, PLE_DIM), norm_w, w1, wup, b_alpha, gla_norm_w, conv_w[0], conv_b,
        wout, wple, wg, b_ple_gate, final_norm_w.reshape(1, D_MODEL),
    )
    return out.reshape(b, s, d)
```
